```python
import math
import jax, jax.numpy as jnp
from jax import lax
import numpy as np

D_MODEL = 2048
BATCH = 16
SEQ = 2048
DEPTH = 2
DEC_BATCH = 4
DEC_SEQ = 4096
PAST_LEN = 128

HEAD_DIM = 128
N_Q_HEADS = 8
N_KV_HEADS = 2
GROUP = N_Q_HEADS // N_KV_HEADS
WINDOW = 128
BLOCK = 128
ROPE_THETA = 10000.0
Q_WIDTH = N_Q_HEADS * HEAD_DIM
KV_WIDTH = N_KV_HEADS * HEAD_DIM

POOL_WINDOWS = (2, 4, 8, 16)
N_POOL_GROUPS = len(POOL_WINDOWS)
POOL_WIDTH = D_MODEL // 2
POOL_GROUP_DIM = POOL_WIDTH // N_POOL_GROUPS

Q_OFF = 0
K_OFF = Q_OFF + Q_WIDTH
V_OFF = K_OFF + KV_WIDTH
U_OFF = V_OFF + KV_WIDTH
G_OFF = U_OFF + POOL_WIDTH
IN_WIDTH = G_OFF + 2 * D_MODEL

D_FF = ((8 * D_MODEL // 3 + 255) // 256) * 256

ALPHA = (2.0 * DEPTH) ** 0.25
BETA = (8.0 * DEPTH) ** -0.25
LN_EPS = 1e-5
NEG_INF = -1e30

kernel_name = "hybrid_winattn_pool_deepnorm_encoder"


def _layer_norm(x, g, b):
    xf = x.astype(jnp.float32)
    mu = jnp.mean(xf, axis=-1, keepdims=True)
    var = jnp.mean(jnp.square(xf - mu), axis=-1, keepdims=True)
    y = (xf - mu) * lax.rsqrt(var + LN_EPS)
    return (y * g.astype(jnp.float32) + b.astype(jnp.float32)).astype(x.dtype)


def _rope(t, pos):
    half = HEAD_DIM // 2
    inv_freq = ROPE_THETA ** (-jnp.arange(half, dtype=jnp.float32) / half)
    ang = pos.astype(jnp.float32)[:, None] * inv_freq[None, :]
    cos = jnp.cos(ang)[None, :, None, :]
    sin = jnp.sin(ang)[None, :, None, :]
    tf = t.astype(jnp.float32)
    t1, t2 = tf[..., :half], tf[..., half:]
    out = jnp.concatenate([t1 * cos - t2 * sin, t2 * cos + t1 * sin], axis=-1)
    return out.astype(t.dtype)


def _band_blocks(t, nb):
    B = t.shape[0]
    tp = jnp.pad(t, ((0, 0), (BLOCK, BLOCK), (0, 0), (0, 0)))
    tp = tp.reshape(B, nb + 2, BLOCK, N_KV_HEADS, HEAD_DIM)
    return jnp.concatenate([tp[:, :-2], tp[:, 1:-1], tp[:, 2:]], axis=2)


def _window_attention(q, k, v, sink):
    B, S = q.shape[0], q.shape[1]
    nb = S // BLOCK
    qb = q.reshape(B, nb, BLOCK, N_KV_HEADS, GROUP, HEAD_DIM)
    kb = _band_blocks(k, nb)
    vb = _band_blocks(v, nb)
    scale = HEAD_DIM ** -0.5
    s = jnp.einsum('bnqhgd,bnkhd->bnhgqk', qb, kb).astype(jnp.float32) * scale
    qi = jnp.arange(BLOCK)
    kj = jnp.arange(3 * BLOCK)
    rel = kj[None, :] - qi[:, None]
    band_ok = (rel >= BLOCK - WINDOW) & (rel <= BLOCK + WINDOW)
    j_abs = jnp.arange(nb)[:, None] * BLOCK - BLOCK + kj[None, :]
    in_range = (j_abs >= 0) & (j_abs < S)
    mask = band_ok[None, :, :] & in_range[:, None, :]
    s = jnp.where(mask[None, :, None, None, :, :], s, NEG_INF)
    sk = sink.astype(jnp.float32).reshape(N_KV_HEADS, GROUP)[None, None, :, :, None, None]
    m = jnp.maximum(jnp.max(s, axis=-1, keepdims=True), sk)
    p = jnp.exp(s - m)
    denom = jnp.sum(p, axis=-1, keepdims=True) + jnp.exp(sk - m)
    o = jnp.einsum('bnhgqk,bnkhd->bnhgqd', p, vb.astype(jnp.float32)) / denom
    o = o.transpose(0, 1, 4, 2, 3, 5).reshape(B, S, Q_WIDTH)
    return o.astype(q.dtype)


def _multiscale_pool(u, w_mix, scale):
    B, S, _ = u.shape
    uf = u.astype(jnp.float32).reshape(B, S, N_POOL_GROUPS, POOL_GROUP_DIM)
    c = jnp.pad(lax.cumsum(uf, axis=1), ((0, 0), (1, 0), (0, 0), (0, 0)))
    pos = jnp.arange(S)
    pooled = []
    for gi, w in enumerate(POOL_WINDOWS):
        lo = jnp.clip(pos - w // 2, 0, S)
        hi = jnp.clip(pos + (w - w // 2), 0, S)
        cg = c[:, :, gi]
        win_sum = jnp.take(cg, hi, axis=1) - jnp.take(cg, lo, axis=1)
        pooled.append(win_sum / (hi - lo).astype(jnp.float32)[None, :, None])
    pooled = jnp.stack(pooled, axis=2) - uf
    mixed = jnp.einsum('bsgc,gcd->bsgd', pooled.astype(u.dtype), w_mix)
    return mixed.reshape(B, S, POOL_WIDTH) * scale


def _layer(x, w_in, sink, w_attn_proj, w_pool_mix, pool_scale, w_pool_proj, w_out,
           ln1_g, ln1_b, w_ffn_in, w_ffn_out, ln2_g, ln2_b):
    B, S, _ = x.shape
    h = x @ w_in
    q = h[..., Q_OFF:K_OFF].reshape(B, S, N_Q_HEADS, HEAD_DIM)
    k = h[..., K_OFF:V_OFF].reshape(B, S, N_KV_HEADS, HEAD_DIM)
    v = h[..., V_OFF:U_OFF].reshape(B, S, N_KV_HEADS, HEAD_DIM)
    u = h[..., U_OFF:G_OFF]
    g_a = h[..., G_OFF:G_OFF + D_MODEL]
    g_b = h[..., G_OFF + D_MODEL:]
    pos = jnp.arange(S)
    q = _rope(q, pos)
    k = _rope(k, pos)
    a = _window_attention(q, k, v, sink) @ w_attn_proj
    b = _multiscale_pool(u, w_pool_mix, pool_scale) @ w_pool_proj
    mix = (jax.nn.sigmoid(g_a) * a + jax.nn.sigmoid(g_b) * b) @ w_out
    x = _layer_norm(ALPHA * x + mix, ln1_g, ln1_b)
    f = x @ w_ffn_in
    f = (jax.nn.silu(f[..., :D_FF]) * f[..., D_FF:]) @ w_ffn_out
    x = _layer_norm(ALPHA * x + f, ln2_g, ln2_b)
    return x


def _trunk(x, w_in, sink, w_attn_proj, w_pool_mix, pool_scale, w_pool_proj, w_out,
           ln1_g, ln1_b, w_ffn_in, w_ffn_out, ln2_g, ln2_b):
    for l in range(DEPTH):
        x = _layer(x, w_in[l], sink[l], w_attn_proj[l], w_pool_mix[l], pool_scale[l],
                   w_pool_proj[l], w_out[l], ln1_g[l], ln1_b[l], w_ffn_in[l],
                   w_ffn_out[l], ln2_g[l], ln2_b[l])
    return x


def setup_inputs(seed: int = 0) -> dict:
    key = jax.random.key(seed)
    ks = jax.random.split(key, 16)
    f32 = jnp.float32
    nrm = lambda k, shape: jax.random.normal(k, shape, dtype=f32)
    x_prompt = nrm(ks[0], (BATCH, SEQ, D_MODEL))
    x_sample = nrm(ks[1], (DEC_BATCH, DEC_SEQ, D_MODEL))
    w_in = nrm(ks[2], (DEPTH, D_MODEL, IN_WIDTH)) * D_MODEL ** -0.5
    w_in = w_in.at[:, :, V_OFF:U_OFF].multiply(BETA)
    sink = nrm(ks[3], (DEPTH, N_Q_HEADS)) * 0.5
    w_attn_proj = nrm(ks[4], (DEPTH, Q_WIDTH, D_MODEL)) * Q_WIDTH ** -0.5
    w_pool_mix = nrm(ks[5], (DEPTH, N_POOL_GROUPS, POOL_GROUP_DIM, POOL_GROUP_DIM)) * POOL_GROUP_DIM ** -0.5
    pool_scale = 1.0 + 0.02 * nrm(ks[6], (DEPTH, POOL_WIDTH))
    w_pool_proj = nrm(ks[7], (DEPTH, POOL_WIDTH, D_MODEL)) * POOL_WIDTH ** -0.5
    w_out = nrm(ks[8], (DEPTH, D_MODEL, D_MODEL)) * (BETA * D_MODEL ** -0.5)
    ln1_g = 1.0 + 0.02 * nrm(ks[9], (DEPTH, D_MODEL))
    ln1_b = 0.02 * nrm(ks[10], (DEPTH, D_MODEL))
    w_ffn_in = nrm(ks[11], (DEPTH, D_MODEL, 2 * D_FF)) * D_MODEL ** -0.5
    w_ffn_out = nrm(ks[12], (DEPTH, D_FF, D_MODEL)) * (BETA * D_FF ** -0.5)
    ln2_g = 1.0 + 0.02 * nrm(ks[13], (DEPTH, D_MODEL))
    ln2_b = 0.02 * nrm(ks[14], (DEPTH, D_MODEL))
    return {"x_prompt": x_prompt, "x_sample": x_sample, "w_in": w_in, "sink": sink,
            "w_attn_proj": w_attn_proj, "w_pool_mix": w_pool_mix, "pool_scale": pool_scale,
            "w_pool_proj": w_pool_proj, "w_out": w_out, "ln1_g": ln1_g, "ln1_b": ln1_b,
            "w_ffn_in": w_ffn_in, "w_ffn_out": w_ffn_out, "ln2_g": ln2_g, "ln2_b": ln2_b}


def reference(x_prompt, x_sample, w_in, sink, w_attn_proj, w_pool_mix, pool_scale,
              w_pool_proj, w_out, ln1_g, ln1_b, w_ffn_in, w_ffn_out, ln2_g, ln2_b):
    y_prompt = _trunk(x_prompt, w_in, sink, w_attn_proj, w_pool_mix, pool_scale,
                      w_pool_proj, w_out, ln1_g, ln1_b, w_ffn_in, w_ffn_out, ln2_g, ln2_b)
    y_sample = _trunk(x_sample, w_in, sink, w_attn_proj, w_pool_mix, pool_scale,
                      w_pool_proj, w_out, ln1_g, ln1_b, w_ffn_in, w_ffn_out, ln2_g, ln2_b)
    return (y_prompt, y_sample)
```

```python
import functools

import jax
import jax.numpy as jnp
from jax import lax
from jax.experimental import pallas as pl
from jax.experimental.pallas import tpu as pltpu

D_MODEL = 2048
DEPTH = 2
HEAD_DIM = 128
N_Q_HEADS = 8
N_KV_HEADS = 2
GROUP = N_Q_HEADS // N_KV_HEADS
WINDOW = 128
BLOCK = 128
ROPE_THETA = 10000.0
Q_WIDTH = N_Q_HEADS * HEAD_DIM
KV_WIDTH = N_KV_HEADS * HEAD_DIM
POOL_WINDOWS = (2, 4, 8, 16)
N_POOL_GROUPS = len(POOL_WINDOWS)
POOL_WIDTH = D_MODEL // 2
POOL_GROUP_DIM = POOL_WIDTH // N_POOL_GROUPS
Q_OFF = 0
K_OFF = Q_OFF + Q_WIDTH
V_OFF = K_OFF + KV_WIDTH
U_OFF = V_OFF + KV_WIDTH
G_OFF = U_OFF + POOL_WIDTH
IN_WIDTH = G_OFF + 2 * D_MODEL
D_FF = ((8 * D_MODEL // 3 + 255) // 256) * 256
ALPHA = (2.0 * DEPTH) ** 0.25
LN_EPS = 1e-5
NEG_INF = -1e30
SM_SCALE = HEAD_DIM ** -0.5

H_GA = 0
H_GB = H_GA + D_MODEL
H_Q = H_GB + D_MODEL
H_U = H_Q + Q_WIDTH
H_KV = H_U + POOL_WIDTH

BF16 = jnp.bfloat16
F32 = jnp.float32

V7X_VMEM_BYTES = 64 * 1024 * 1024
VMEM_LIMIT_BYTES = 56 * 1024 * 1024
POOL_HALO = 16


def _params(n_axes):
    return pltpu.CompilerParams(
        dimension_semantics=("arbitrary",) * n_axes,
        vmem_limit_bytes=VMEM_LIMIT_BYTES,
    )


def _layer_norm(y, g, b):
    mu = jnp.mean(y, axis=-1, keepdims=True)
    yc = y - mu
    var = jnp.mean(yc * yc, axis=-1, keepdims=True)
    return yc * lax.rsqrt(var + LN_EPS) * g + b


IN_TN = 512
_IN_GATE_BLOCKS = (2 * D_MODEL) // IN_TN
_IN_Q_END = _IN_GATE_BLOCKS + Q_WIDTH // IN_TN
_IN_U_END = _IN_Q_END + POOL_WIDTH // IN_TN


def _in_proj_kernel(x_ref, w_ref, cos_ref, sin_ref, h_ref, xb_ref):
    j = pl.program_id(1)

    @pl.when(j == 0)
    def _():
        xb_ref[...] = x_ref[...].astype(BF16)

    acc = jnp.dot(xb_ref[...], w_ref[...], preferred_element_type=F32)

    def rope(t):
        return t * cos_ref[...] + pltpu.roll(t, HEAD_DIM // 2, 1) * sin_ref[...]

    @pl.when(j < _IN_GATE_BLOCKS)
    def _():
        h_ref[...] = jax.nn.sigmoid(acc).astype(BF16)

    @pl.when((j >= _IN_GATE_BLOCKS) & (j < _IN_Q_END))
    def _():
        for c in range(IN_TN // HEAD_DIM):
            cs = slice(c * HEAD_DIM, (c + 1) * HEAD_DIM)
            h_ref[:, cs] = (rope(acc[:, cs]) * SM_SCALE).astype(BF16)

    @pl.when((j >= _IN_Q_END) & (j < _IN_U_END))
    def _():
        h_ref[...] = acc.astype(BF16)

    @pl.when(j >= _IN_U_END)
    def _():
        for c in range(N_KV_HEADS):
            cs = slice(c * HEAD_DIM, (c + 1) * HEAD_DIM)
            h_ref[:, cs] = rope(acc[:, cs]).astype(BF16)
        h_ref[:, KV_WIDTH:] = acc[:, KV_WIDTH:].astype(BF16)


def _in_proj(x, w, cos, sin, seq, tm):
    t = x.shape[0]
    tiles_per_seq = seq // tm
    return pl.pallas_call(
        _in_proj_kernel,
        grid=(t // tm, IN_WIDTH // IN_TN),
        in_specs=[
            pl.BlockSpec((tm, D_MODEL), lambda i, j: (i, 0)),
            pl.BlockSpec((D_MODEL, IN_TN), lambda i, j: (0, j)),
            pl.BlockSpec((tm, HEAD_DIM), lambda i, j: (i % tiles_per_seq, 0)),
            pl.BlockSpec((tm, HEAD_DIM), lambda i, j: (i % tiles_per_seq, 0)),
        ],
        out_specs=pl.BlockSpec((tm, IN_TN), lambda i, j: (i, j)),
        out_shape=jax.ShapeDtypeStruct((t, IN_WIDTH), BF16),
        scratch_shapes=[pltpu.VMEM((tm, D_MODEL), BF16)],
        compiler_params=_params(2),
        name="in_proj",
    )(x, w, cos, sin)


def _attn_kernel(sink_ref, q_ref, kvp_ref, kvm_ref, kvn_ref, o_ref, kv_ref, *, tq, tiles_per_seq):
    t = pl.program_id(0) % tiles_per_seq
    is_first = t == 0
    is_last = t == tiles_per_seq - 1
    nb = tq // BLOCK
    kw = 3 * BLOCK

    kv_ref[0:BLOCK, :] = kvp_ref[...]
    kv_ref[BLOCK:BLOCK + tq, :] = kvm_ref[...]
    kv_ref[BLOCK + tq:, :] = kvn_ref[...]

    rows = GROUP * BLOCK
    qi = lax.broadcasted_iota(jnp.int32, (rows, kw), 0) % BLOCK
    kj = lax.broadcasted_iota(jnp.int32, (rows, kw), 1)
    rel = kj - qi
    band = (rel >= BLOCK - WINDOW) & (rel <= BLOCK + WINDOW)
    head_of_row = lax.broadcasted_iota(jnp.int32, (rows, 1), 0) // BLOCK

    for b in range(nb):
        mask = band
        if b == 0:
            mask = mask & (kj >= jnp.where(is_first, BLOCK, 0))
        if b == nb - 1:
            mask = mask & (kj < jnp.where(is_last, 2 * BLOCK, kw))
        rs = slice(b * BLOCK, (b + 1) * BLOCK)
        for g in range(N_KV_HEADS):
            k = kv_ref[b * BLOCK:b * BLOCK + kw, g * HEAD_DIM:(g + 1) * HEAD_DIM]
            v = kv_ref[b * BLOCK:b * BLOCK + kw, KV_WIDTH + g * HEAD_DIM:KV_WIDTH + (g + 1) * HEAD_DIM]
            q = jnp.concatenate(
                [q_ref[rs, (g * GROUP + hh) * HEAD_DIM:(g * GROUP + hh + 1) * HEAD_DIM] for hh in range(GROUP)],
                axis=0)
            sk = jnp.zeros((rows, 1), F32)
            for hh in range(GROUP):
                sk = jnp.where(head_of_row == hh, sink_ref[g * GROUP + hh], sk)
            s = lax.dot_general(q, k, (((1,), (1,)), ((), ())), preferred_element_type=F32)
            s = jnp.where(mask, s, NEG_INF)
            m = jnp.maximum(jnp.max(s, axis=-1, keepdims=True), sk)
            p = jnp.exp(s - m)
            denom = jnp.sum(p, axis=-1, keepdims=True) + jnp.exp(sk - m)
            o = jnp.dot(p.astype(BF16), v, preferred_element_type=F32) / denom
            for hh in range(GROUP):
                h = g * GROUP + hh
                o_ref[rs, h * HEAD_DIM:(h + 1) * HEAD_DIM] = o[hh * BLOCK:(hh + 1) * BLOCK].astype(BF16)


def _win_attn(h, sink, seq, tq):
    t = h.shape[0]
    tiles_per_seq = seq // tq
    nblk = t // BLOCK
    bpt = tq // BLOCK
    kv_col = H_KV // (2 * KV_WIDTH)
    return pl.pallas_call(
        functools.partial(_attn_kernel, tq=tq, tiles_per_seq=tiles_per_seq),
        grid=(t // tq,),
        in_specs=[
            pl.BlockSpec(memory_space=pltpu.SMEM),
            pl.BlockSpec((tq, Q_WIDTH), lambda i: (i, H_Q // Q_WIDTH)),
            pl.BlockSpec((BLOCK, 2 * KV_WIDTH), lambda i: (jnp.maximum(i * bpt - 1, 0), kv_col)),
            pl.BlockSpec((tq, 2 * KV_WIDTH), lambda i: (i, kv_col)),
            pl.BlockSpec((BLOCK, 2 * KV_WIDTH), lambda i: (jnp.minimum((i + 1) * bpt, nblk - 1), kv_col)),
        ],
        out_specs=pl.BlockSpec((tq, Q_WIDTH), lambda i: (i, 0)),
        out_shape=jax.ShapeDtypeStruct((t, Q_WIDTH), BF16),
        scratch_shapes=[pltpu.VMEM((tq + 2 * BLOCK, 2 * KV_WIDTH), BF16)],
        compiler_params=_params(1),
        name="win_attn",
    )(sink, h, h, h, h)


def _pool_kernel(up_ref, um_ref, un_ref, wmix_ref, scale_ref, pm_ref, u_ref, *, tp, seq, tiles_per_seq):
    t = pl.program_id(0) % tiles_per_seq
    u_ref[0:POOL_HALO, :] = jnp.where(t == 0, 0.0, up_ref[...].astype(F32))
    u_ref[POOL_HALO:POOL_HALO + tp, :] = um_ref[...].astype(F32)
    u_ref[POOL_HALO + tp:, :] = jnp.where(t == tiles_per_seq - 1, 0.0, un_ref[...].astype(F32))

    pos = t * tp + lax.broadcasted_iota(jnp.int32, (tp, 1), 0)
    for g, w in enumerate(POOL_WINDOWS):
        half = w // 2
        cs = slice(g * POOL_GROUP_DIM, (g + 1) * POOL_GROUP_DIM)
        win = u_ref[POOL_HALO - half:POOL_HALO - half + tp, cs]
        for d in range(-half + 1, half):
            win = win + u_ref[POOL_HALO + d:POOL_HALO + d + tp, cs]
        cnt = jnp.minimum(pos + half, seq) - jnp.maximum(pos - half, 0)
        inv = 1.0 / cnt.astype(F32)
        pooled = win * inv - u_ref[POOL_HALO:POOL_HALO + tp, cs]
        mixed = jnp.dot(pooled.astype(BF16), wmix_ref[g], preferred_element_type=F32)
        pm_ref[:, cs] = (mixed * scale_ref[:, cs]).astype(BF16)


def _pool_mix(h, w_mix, scale, seq, tp):
    t = h.shape[0]
    tiles_per_seq = seq // tp
    hpt = tp // POOL_HALO
    nhalo = t // POOL_HALO
    u_col = H_U // POOL_WIDTH
    return pl.pallas_call(
        functools.partial(_pool_kernel, tp=tp, seq=seq, tiles_per_seq=tiles_per_seq),
        grid=(t // tp,),
        in_specs=[
            pl.BlockSpec((POOL_HALO, POOL_WIDTH), lambda i: (jnp.maximum(i * hpt - 1, 0), u_col)),
            pl.BlockSpec((tp, POOL_WIDTH), lambda i: (i, u_col)),
            pl.BlockSpec((POOL_HALO, POOL_WIDTH), lambda i: (jnp.minimum((i + 1) * hpt, nhalo - 1), u_col)),
            pl.BlockSpec((N_POOL_GROUPS, POOL_GROUP_DIM, POOL_GROUP_DIM), lambda i: (0, 0, 0)),
            pl.BlockSpec((1, POOL_WIDTH), lambda i: (0, 0)),
        ],
        out_specs=pl.BlockSpec((tp, POOL_WIDTH), lambda i: (i, 0)),
        out_shape=jax.ShapeDtypeStruct((t, POOL_WIDTH), BF16),
        scratch_shapes=[pltpu.VMEM((tp + 2 * POOL_HALO, POOL_WIDTH), F32)],
        compiler_params=_params(1),
        name="pool_mix",
    )(h, h, h, w_mix, scale)


def _mix_kernel(o_ref, pm_ref, sga_ref, sgb_ref, x_ref, wap_ref, wpp_ref, wout_ref, g_ref, b_ref, y_ref):
    a = jnp.dot(o_ref[...], wap_ref[...], preferred_element_type=F32)
    b = jnp.dot(pm_ref[...], wpp_ref[...], preferred_element_type=F32)
    gated = sga_ref[...].astype(F32) * a + sgb_ref[...].astype(F32) * b
    mix = jnp.dot(gated.astype(BF16), wout_ref[...], preferred_element_type=F32)
    y_ref[...] = _layer_norm(ALPHA * x_ref[...] + mix, g_ref[...], b_ref[...])


def _mix_ln(o, pm, h, x, wap, wpp, wout, g, b, tm):
    t = x.shape[0]
    resident = dict(pipeline_mode=pl.Buffered(1))
    return pl.pallas_call(
        _mix_kernel,
        grid=(t // tm,),
        in_specs=[
            pl.BlockSpec((tm, Q_WIDTH), lambda i: (i, 0)),
            pl.BlockSpec((tm, POOL_WIDTH), lambda i: (i, 0)),
            pl.BlockSpec((tm, D_MODEL), lambda i: (i, H_GA // D_MODEL)),
            pl.BlockSpec((tm, D_MODEL), lambda i: (i, H_GB // D_MODEL)),
            pl.BlockSpec((tm, D_MODEL), lambda i: (i, 0)),
            pl.BlockSpec((Q_WIDTH, D_MODEL), lambda i: (0, 0), **resident),
            pl.BlockSpec((POOL_WIDTH, D_MODEL), lambda i: (0, 0), **resident),
            pl.BlockSpec((D_MODEL, D_MODEL), lambda i: (0, 0), **resident),
            pl.BlockSpec((1, D_MODEL), lambda i: (0, 0)),
            pl.BlockSpec((1, D_MODEL), lambda i: (0, 0)),
        ],
        out_specs=pl.BlockSpec((tm, D_MODEL), lambda i: (i, 0)),
        out_shape=jax.ShapeDtypeStruct((t, D_MODEL), F32),
        compiler_params=_params(1),
        name="mix_ln",
    )(o, pm, h, h, x, wap, wpp, wout, g, b)


FFN_TF = 512
_FFN_STEPS = D_FF // FFN_TF


def _ffn_kernel(x_ref, wg_ref, wu_ref, wd_ref, g_ref, b_ref, y_ref, xb_ref):
    c = pl.program_id(1)

    @pl.when(c == 0)
    def _():
        xb_ref[...] = x_ref[...].astype(BF16)

    xb = xb_ref[...]
    gate = jnp.dot(xb, wg_ref[...], preferred_element_type=F32)
    up = jnp.dot(xb, wu_ref[...], preferred_element_type=F32)
    mid = (gate * jax.nn.sigmoid(gate) * up).astype(BF16)
    part = jnp.dot(mid, wd_ref[...], preferred_element_type=F32)

    @pl.when(c == 0)
    def _():
        y_ref[...] = part

    @pl.when((c > 0) & (c < _FFN_STEPS - 1))
    def _():
        y_ref[...] += part

    @pl.when(c == _FFN_STEPS - 1)
    def _():
        f = y_ref[...] + part
        y_ref[...] = _layer_norm(ALPHA * x_ref[...] + f, g_ref[...], b_ref[...])


def _ffn_ln(x, w_in, w_out, g, b, tm):
    t = x.shape[0]
    return pl.pallas_call(
        _ffn_kernel,
        grid=(t // tm, _FFN_STEPS),
        in_specs=[
            pl.BlockSpec((tm, D_MODEL), lambda i, c: (i, 0)),
            pl.BlockSpec((D_MODEL, FFN_TF), lambda i, c: (0, c)),
            pl.BlockSpec((D_MODEL, FFN_TF), lambda i, c: (0, _FFN_STEPS + c)),
            pl.BlockSpec((FFN_TF, D_MODEL), lambda i, c: (c, 0)),
            pl.BlockSpec((1, D_MODEL), lambda i, c: (0, 0)),
            pl.BlockSpec((1, D_MODEL), lambda i, c: (0, 0)),
        ],
        out_specs=pl.BlockSpec((tm, D_MODEL), lambda i, c: (i, 0)),
        out_shape=jax.ShapeDtypeStruct((t, D_MODEL), F32),
        scratch_shapes=[pltpu.VMEM((tm, D_MODEL), BF16)],
        compiler_params=_params(2),
        name="ffn_ln",
    )(x, w_in, w_in, w_out, g, b)


IN_TM = 1024
ATTN_TQ = 512
POOL_TP = 512
MIX_TM = 256
FFN_TM = 512


def _rope_tables(seq):
    half = HEAD_DIM // 2
    inv_freq = ROPE_THETA ** (-jnp.arange(half, dtype=F32) / half)
    ang = jnp.arange(seq, dtype=F32)[:, None] * inv_freq[None, :]
    cos, sin = jnp.cos(ang), jnp.sin(ang)
    return jnp.concatenate([cos, cos], axis=1), jnp.concatenate([-sin, sin], axis=1)


def _trunk(x, p):
    bsz, seq, _ = x.shape
    x = x.reshape(bsz * seq, D_MODEL)
    cos, sin = _rope_tables(seq)
    for l in range(DEPTH):
        h = _in_proj(x, p["w_in"][l], cos, sin, seq, IN_TM)
        o = _win_attn(h, p["sink"][l], seq, ATTN_TQ)
        pm = _pool_mix(h, p["w_pool_mix"][l], p["pool_scale"][l], seq, POOL_TP)
        x = _mix_ln(o, pm, h, x, p["w_attn_proj"][l], p["w_pool_proj"][l], p["w_out"][l],
                    p["ln1_g"][l], p["ln1_b"][l], MIX_TM)
        x = _ffn_ln(x, p["w_ffn_in"][l], p["w_ffn_out"][l], p["ln2_g"][l], p["ln2_b"][l], FFN_TM)
    return x.reshape(bsz, seq, D_MODEL)


def kernel(x_prompt, x_sample, w_in, sink, w_attn_proj, w_pool_mix, pool_scale, w_pool_proj, w_out,
           ln1_g, ln1_b, w_ffn_in, w_ffn_out, ln2_g, ln2_b):
    w_in_p = jnp.concatenate(
        [w_in[:, :, G_OFF:], w_in[:, :, Q_OFF:K_OFF], w_in[:, :, U_OFF:G_OFF], w_in[:, :, K_OFF:U_OFF]],
        axis=2).astype(BF16)
    p = {
        "w_in": w_in_p,
        "sink": sink.astype(F32),
        "w_attn_proj": w_attn_proj.astype(BF16),
        "w_pool_mix": w_pool_mix.astype(BF16),
        "pool_scale": pool_scale.astype(F32).reshape(DEPTH, 1, POOL_WIDTH),
        "w_pool_proj": w_pool_proj.astype(BF16),
        "w_out": w_out.astype(BF16),
        "ln1_g": ln1_g.astype(F32).reshape(DEPTH, 1, D_MODEL),
        "ln1_b": ln1_b.astype(F32).reshape(DEPTH, 1, D_MODEL),
        "w_ffn_in": w_ffn_in.astype(BF16),
        "w_ffn_out": w_ffn_out.astype(BF16),
        "ln2_g": ln2_g.astype(F32).reshape(DEPTH, 1, D_MODEL),
        "ln2_b": ln2_b.astype(F32).reshape(DEPTH, 1, D_MODEL),
    }
    return (_trunk(x_prompt, p), _trunk(x_sample, p))
```

```python
import functools

import jax
import jax.numpy as jnp
from jax import lax
from jax.experimental import pallas as pl
from jax.experimental.pallas import tpu as pltpu

D_MODEL = 2048
DEPTH = 2
HEAD_DIM = 128
N_Q_HEADS = 8
N_KV_HEADS = 2
GROUP = N_Q_HEADS // N_KV_HEADS
WINDOW = 128
BLOCK = 128
ROPE_THETA = 10000.0
Q_WIDTH = N_Q_HEADS * HEAD_DIM
KV_WIDTH = N_KV_HEADS * HEAD_DIM
POOL_WINDOWS = (2, 4, 8, 16)
N_POOL_GROUPS = len(POOL_WINDOWS)
POOL_WIDTH = D_MODEL // 2
POOL_GROUP_DIM = POOL_WIDTH // N_POOL_GROUPS
Q_OFF = 0
K_OFF = Q_OFF + Q_WIDTH
V_OFF = K_OFF + KV_WIDTH
U_OFF = V_OFF + KV_WIDTH
G_OFF = U_OFF + POOL_WIDTH
IN_WIDTH = G_OFF + 2 * D_MODEL
D_FF = ((8 * D_MODEL // 3 + 255) // 256) * 256
ALPHA = (2.0 * DEPTH) ** 0.25
LN_EPS = 1e-5
NEG_INF = -1e30
SM_SCALE = HEAD_DIM ** -0.5

H_GA = 0
H_GB = H_GA + D_MODEL
H_Q = H_GB + D_MODEL
H_U = H_Q + Q_WIDTH
H_K = H_U + POOL_WIDTH
H_V = H_K + KV_WIDTH

BF16 = jnp.bfloat16
F32 = jnp.float32

V7X_VMEM_BYTES = 64 * 1024 * 1024
VMEM_LIMIT_BYTES = V7X_VMEM_BYTES - 6 * 1024 * 1024
V7X_MXU_DIM = 256
POOL_HALO = 16


def _params(n_axes):
    return pltpu.CompilerParams(
        dimension_semantics=("arbitrary",) * n_axes,
        vmem_limit_bytes=VMEM_LIMIT_BYTES,
    )


def _resident(shape):
    return pl.BlockSpec(shape, lambda *_: (0,) * len(shape), pipeline_mode=pl.Buffered(1))


def _layer_norm(y, g, b):
    mu = jnp.mean(y, axis=-1, keepdims=True)
    yc = y - mu
    var = jnp.mean(yc * yc, axis=-1, keepdims=True)
    return yc * lax.rsqrt(var + LN_EPS) * g + b


def _in_proj_kernel(x_ref, w_ref, cos_ref, sin_ref, h_ref, xb_ref):
    xb_ref[...] = x_ref[...].astype(BF16)

    def rope(t):
        return t * cos_ref[...] + pltpu.roll(t, HEAD_DIM // 2, 1) * sin_ref[...]

    def per_head(fn, acc):
        return jnp.concatenate(
            [fn(acc[:, c:c + HEAD_DIM]) for c in range(0, acc.shape[1], HEAD_DIM)], axis=1)

    for c0 in range(0, IN_WIDTH, V7X_MXU_DIM):
        cs = slice(c0, c0 + V7X_MXU_DIM)
        acc = jnp.dot(xb_ref[...], w_ref[:, cs], preferred_element_type=F32)
        if c0 < H_Q:
            out = jax.nn.sigmoid(acc)
        elif c0 < H_U:
            out = per_head(lambda t: rope(t) * SM_SCALE, acc)
        elif H_K <= c0 < H_V:
            out = per_head(rope, acc)
        else:
            out = acc
        h_ref[:, cs] = out.astype(BF16)


def _in_proj(x, w, cos, sin, seq, tm):
    t = x.shape[0]
    tiles_per_seq = seq // tm
    return pl.pallas_call(
        _in_proj_kernel,
        grid=(t // tm,),
        in_specs=[
            pl.BlockSpec((tm, D_MODEL), lambda i: (i, 0)),
            _resident((D_MODEL, IN_WIDTH)),
            pl.BlockSpec((tm, HEAD_DIM), lambda i: (i % tiles_per_seq, 0)),
            pl.BlockSpec((tm, HEAD_DIM), lambda i: (i % tiles_per_seq, 0)),
        ],
        out_specs=pl.BlockSpec((tm, IN_WIDTH), lambda i: (i, 0)),
        out_shape=jax.ShapeDtypeStruct((t, IN_WIDTH), BF16),
        scratch_shapes=[pltpu.VMEM((tm, D_MODEL), BF16)],
        compiler_params=_params(1),
        name="in_proj",
    )(x, w, cos, sin)


def _attn_kernel(sink_ref, q_ref, kvp_ref, kvm_ref, kvn_ref, o_ref, kv_ref, *, tq, tiles_per_seq):
    t = pl.program_id(0) % tiles_per_seq
    is_first = t == 0
    is_last = t == tiles_per_seq - 1
    nb = tq // BLOCK
    kw = 3 * BLOCK

    kv_ref[0:BLOCK, :] = kvp_ref[...]
    kv_ref[BLOCK:BLOCK + tq, :] = kvm_ref[...]
    kv_ref[BLOCK + tq:, :] = kvn_ref[...]

    rows = GROUP * BLOCK
    qi = lax.broadcasted_iota(jnp.int32, (rows, kw), 0) % BLOCK
    kj = lax.broadcasted_iota(jnp.int32, (rows, kw), 1)
    rel = kj - qi
    band = (rel >= BLOCK - WINDOW) & (rel <= BLOCK + WINDOW)
    head_of_row = lax.broadcasted_iota(jnp.int32, (rows, 1), 0) // BLOCK

    for b in range(nb):
        mask = band
        if b == 0:
            mask = mask & (kj >= jnp.where(is_first, BLOCK, 0))
        if b == nb - 1:
            mask = mask & (kj < jnp.where(is_last, 2 * BLOCK, kw))
        rs = slice(b * BLOCK, (b + 1) * BLOCK)
        for g in range(N_KV_HEADS):
            k = kv_ref[b * BLOCK:b * BLOCK + kw, g * HEAD_DIM:(g + 1) * HEAD_DIM]
            v = kv_ref[b * BLOCK:b * BLOCK + kw, KV_WIDTH + g * HEAD_DIM:KV_WIDTH + (g + 1) * HEAD_DIM]
            q = jnp.concatenate(
                [q_ref[rs, (g * GROUP + hh) * HEAD_DIM:(g * GROUP + hh + 1) * HEAD_DIM] for hh in range(GROUP)],
                axis=0)
            sk = jnp.zeros((rows, 1), F32)
            for hh in range(GROUP):
                sk = jnp.where(head_of_row == hh, sink_ref[g * GROUP + hh], sk)
            s = lax.dot_general(q, k, (((1,), (1,)), ((), ())), preferred_element_type=F32)
            s = jnp.where(mask, s, NEG_INF)
            m = jnp.maximum(jnp.max(s, axis=-1, keepdims=True), sk)
            p = jnp.exp(s - m)
            denom = jnp.sum(p, axis=-1, keepdims=True) + jnp.exp(sk - m)
            o = jnp.dot(p.astype(BF16), v, preferred_element_type=F32) / denom
            for hh in range(GROUP):
                h = g * GROUP + hh
                o_ref[rs, h * HEAD_DIM:(h + 1) * HEAD_DIM] = o[hh * BLOCK:(hh + 1) * BLOCK].astype(BF16)


def _win_attn(h, sink, seq, tq):
    t = h.shape[0]
    tiles_per_seq = seq // tq
    nblk = t // BLOCK
    bpt = tq // BLOCK
    kv_col = H_K // (2 * KV_WIDTH)
    return pl.pallas_call(
        functools.partial(_attn_kernel, tq=tq, tiles_per_seq=tiles_per_seq),
        grid=(t // tq,),
        in_specs=[
            pl.BlockSpec(memory_space=pltpu.SMEM),
            pl.BlockSpec((tq, Q_WIDTH), lambda i: (i, H_Q // Q_WIDTH)),
            pl.BlockSpec((BLOCK, 2 * KV_WIDTH), lambda i: (jnp.maximum(i * bpt - 1, 0), kv_col)),
            pl.BlockSpec((tq, 2 * KV_WIDTH), lambda i: (i, kv_col)),
            pl.BlockSpec((BLOCK, 2 * KV_WIDTH), lambda i: (jnp.minimum((i + 1) * bpt, nblk - 1), kv_col)),
        ],
        out_specs=pl.BlockSpec((tq, Q_WIDTH), lambda i: (i, 0)),
        out_shape=jax.ShapeDtypeStruct((t, Q_WIDTH), BF16),
        scratch_shapes=[pltpu.VMEM((tq + 2 * BLOCK, 2 * KV_WIDTH), BF16)],
        compiler_params=_params(1),
        name="win_attn",
    )(sink, h, h, h, h)


def _pool_kernel(up_ref, um_ref, un_ref, wmix_ref, scale_ref, pm_ref, *, tp, seq, tiles_per_seq):
    t = pl.program_id(0) % tiles_per_seq
    n = tp + 2 * POOL_HALO
    u = jnp.concatenate([
        jnp.where(t == 0, 0.0, up_ref[...].astype(F32)),
        um_ref[...].astype(F32),
        jnp.where(t == tiles_per_seq - 1, 0.0, un_ref[...].astype(F32)),
    ], axis=0)

    def back(a, s):
        return pltpu.roll(a, s, 0)

    pos = t * tp + lax.broadcasted_iota(jnp.int32, (tp, 1), 0)
    for g, w in enumerate(POOL_WINDOWS):
        half = w // 2
        cs = slice(g * POOL_GROUP_DIM, (g + 1) * POOL_GROUP_DIM)
        ug = u[:, cs]
        d, span = ug, 1
        while span < w:
            d = d + back(d, span)
            span *= 2
        if half > 1:
            d = back(d, n - (half - 1))
        win = d[POOL_HALO:POOL_HALO + tp]
        cnt = jnp.minimum(pos + half, seq) - jnp.maximum(pos - half, 0)
        inv = 1.0 / cnt.astype(F32)
        pooled = win * inv - ug[POOL_HALO:POOL_HALO + tp]
        mixed = jnp.dot(pooled.astype(BF16), wmix_ref[g], preferred_element_type=F32)
        pm_ref[:, cs] = (mixed * scale_ref[:, cs]).astype(BF16)


def _pool_mix(h, w_mix, scale, seq, tp):
    t = h.shape[0]
    tiles_per_seq = seq // tp
    hpt = tp // POOL_HALO
    nhalo = t // POOL_HALO
    u_col = H_U // POOL_WIDTH
    return pl.pallas_call(
        functools.partial(_pool_kernel, tp=tp, seq=seq, tiles_per_seq=tiles_per_seq),
        grid=(t // tp,),
        in_specs=[
            pl.BlockSpec((POOL_HALO, POOL_WIDTH), lambda i: (jnp.maximum(i * hpt - 1, 0), u_col)),
            pl.BlockSpec((tp, POOL_WIDTH), lambda i: (i, u_col)),
            pl.BlockSpec((POOL_HALO, POOL_WIDTH), lambda i: (jnp.minimum((i + 1) * hpt, nhalo - 1), u_col)),
            pl.BlockSpec((N_POOL_GROUPS, POOL_GROUP_DIM, POOL_GROUP_DIM), lambda i: (0, 0, 0)),
            pl.BlockSpec((1, POOL_WIDTH), lambda i: (0, 0)),
        ],
        out_specs=pl.BlockSpec((tp, POOL_WIDTH), lambda i: (i, 0)),
        out_shape=jax.ShapeDtypeStruct((t, POOL_WIDTH), BF16),
        compiler_params=_params(1),
        name="pool_mix",
    )(h, h, h, w_mix, scale)


MIX_SUB = 256


def _mix_kernel(o_ref, pm_ref, sga_ref, sgb_ref, x_ref, wap_ref, wpp_ref, wout_ref, g_ref, b_ref,
                y_ref, yb_ref):
    for r0 in range(0, o_ref.shape[0], MIX_SUB):
        rs = slice(r0, r0 + MIX_SUB)
        a = jnp.dot(o_ref[rs, :], wap_ref[...], preferred_element_type=F32)
        b = jnp.dot(pm_ref[rs, :], wpp_ref[...], preferred_element_type=F32)
        gated = sga_ref[rs, :].astype(F32) * a + sgb_ref[rs, :].astype(F32) * b
        mix = jnp.dot(gated.astype(BF16), wout_ref[...], preferred_element_type=F32)
        y = _layer_norm(ALPHA * x_ref[rs, :] + mix, g_ref[...], b_ref[...])
        y_ref[rs, :] = y
        yb_ref[rs, :] = y.astype(BF16)


def _mix_ln(o, pm, h, x, wap, wpp, wout, g, b, tm):
    t = x.shape[0]
    row = lambda i: (i, 0)
    return pl.pallas_call(
        _mix_kernel,
        grid=(t // tm,),
        in_specs=[
            pl.BlockSpec((tm, Q_WIDTH), row),
            pl.BlockSpec((tm, POOL_WIDTH), row),
            pl.BlockSpec((tm, D_MODEL), lambda i: (i, H_GA // D_MODEL)),
            pl.BlockSpec((tm, D_MODEL), lambda i: (i, H_GB // D_MODEL)),
            pl.BlockSpec((tm, D_MODEL), row),
            _resident((Q_WIDTH, D_MODEL)),
            _resident((POOL_WIDTH, D_MODEL)),
            _resident((D_MODEL, D_MODEL)),
            _resident((1, D_MODEL)),
            _resident((1, D_MODEL)),
        ],
        out_specs=[pl.BlockSpec((tm, D_MODEL), row), pl.BlockSpec((tm, D_MODEL), row)],
        out_shape=[jax.ShapeDtypeStruct((t, D_MODEL), F32), jax.ShapeDtypeStruct((t, D_MODEL), BF16)],
        compiler_params=_params(1),
        name="mix_ln",
    )(o, pm, h, h, x, wap, wpp, wout, g, b)


FFN_TF = 512
_FFN_STEPS = D_FF // FFN_TF


def _ffn_kernel(x_ref, xb_ref, wg_ref, wu_ref, wd_ref, g_ref, b_ref, y_ref):
    c = pl.program_id(1)
    xb = xb_ref[...]
    gate = jnp.dot(xb, wg_ref[...], preferred_element_type=F32)
    up = jnp.dot(xb, wu_ref[...], preferred_element_type=F32)
    mid = (gate * jax.nn.sigmoid(gate) * up).astype(BF16)
    part = jnp.dot(mid, wd_ref[...], preferred_element_type=F32)
    y_ref[...] = jnp.where(c == 0, 0.0, y_ref[...]) + part

    @pl.when(c == _FFN_STEPS - 1)
    def _():
        y_ref[...] = _layer_norm(ALPHA * x_ref[...] + y_ref[...], g_ref[...], b_ref[...])


def _ffn_ln(x, xb, w_in, w_out, g, b, tm):
    t = x.shape[0]
    row = lambda i, c: (i, 0)
    return pl.pallas_call(
        _ffn_kernel,
        grid=(t // tm, _FFN_STEPS),
        in_specs=[
            pl.BlockSpec((tm, D_MODEL), row),
            pl.BlockSpec((tm, D_MODEL), row),
            pl.BlockSpec((D_MODEL, FFN_TF), lambda i, c: (0, c)),
            pl.BlockSpec((D_MODEL, FFN_TF), lambda i, c: (0, _FFN_STEPS + c)),
            pl.BlockSpec((FFN_TF, D_MODEL), lambda i, c: (c, 0)),
            _resident((1, D_MODEL)),
            _resident((1, D_MODEL)),
        ],
        out_specs=pl.BlockSpec((tm, D_MODEL), row),
        out_shape=jax.ShapeDtypeStruct((t, D_MODEL), F32),
        compiler_params=_params(2),
        name="ffn_ln",
    )(x, xb, w_in, w_in, w_out, g, b)


TILES = dict(in_proj=512, win_attn=512, pool_mix=512, mix_ln=512, ffn_ln=512)


def _rope_tables(seq):
    half = HEAD_DIM // 2
    inv_freq = ROPE_THETA ** (-jnp.arange(half, dtype=F32) / half)
    ang = jnp.arange(seq, dtype=F32)[:, None] * inv_freq[None, :]
    cos, sin = jnp.cos(ang), jnp.sin(ang)
    return jnp.concatenate([cos, cos], axis=1), jnp.concatenate([-sin, sin], axis=1)


def _trunk(x, p, tiles):
    bsz, seq, _ = x.shape
    x = x.reshape(bsz * seq, D_MODEL)
    cos, sin = _rope_tables(seq)
    for l in range(DEPTH):
        h = _in_proj(x, p["w_in"][l], cos, sin, seq, tiles["in_proj"])
        o = _win_attn(h, p["sink"][l], seq, tiles["win_attn"])
        pm = _pool_mix(h, p["w_pool_mix"][l], p["pool_scale"][l], seq, tiles["pool_mix"])
        x, xb = _mix_ln(o, pm, h, x, p["w_attn_proj"][l], p["w_pool_proj"][l], p["w_out"][l],
                        p["ln1_g"][l], p["ln1_b"][l], tiles["mix_ln"])
        x = _ffn_ln(x, xb, p["w_ffn_in"][l], p["w_ffn_out"][l], p["ln2_g"][l], p["ln2_b"][l],
                    tiles["ffn_ln"])
    return x.reshape(bsz, seq, D_MODEL)


def _prepare_params(w_in, sink, w_attn_proj, w_pool_mix, pool_scale, w_pool_proj, w_out,
                    ln1_g, ln1_b, w_ffn_in, w_ffn_out, ln2_g, ln2_b):
    w_in_p = jnp.concatenate(
        [w_in[:, :, G_OFF:], w_in[:, :, Q_OFF:K_OFF], w_in[:, :, U_OFF:G_OFF], w_in[:, :, K_OFF:U_OFF]],
        axis=2).astype(BF16)
    depth = w_in.shape[0]
    vec = lambda a: a.astype(F32).reshape(depth, 1, -1)
    return {
        "w_in": w_in_p,
        "sink": sink.astype(F32),
        "w_attn_proj": w_attn_proj.astype(BF16),
        "w_pool_mix": w_pool_mix.astype(BF16),
        "pool_scale": vec(pool_scale),
        "w_pool_proj": w_pool_proj.astype(BF16),
        "w_out": w_out.astype(BF16),
        "ln1_g": vec(ln1_g), "ln1_b": vec(ln1_b),
        "w_ffn_in": w_ffn_in.astype(BF16),
        "w_ffn_out": w_ffn_out.astype(BF16),
        "ln2_g": vec(ln2_g), "ln2_b": vec(ln2_b),
    }


def kernel(x_prompt, x_sample, w_in, sink, w_attn_proj, w_pool_mix, pool_scale, w_pool_proj, w_out,
           ln1_g, ln1_b, w_ffn_in, w_ffn_out, ln2_g, ln2_b):
    p = _prepare_params(w_in, sink, w_attn_proj, w_pool_mix, pool_scale, w_pool_proj, w_out,
                        ln1_g, ln1_b, w_ffn_in, w_ffn_out, ln2_g, ln2_b)
    return (_trunk(x_prompt, p, TILES), _trunk(x_sample, p, TILES))
```

```python
import functools

import jax
import jax.numpy as jnp
from jax import lax
from jax.experimental import pallas as pl
from jax.experimental.pallas import tpu as pltpu

D_MODEL = 2048
DEPTH = 2
HEAD_DIM = 128
N_Q_HEADS = 8
N_KV_HEADS = 2
GROUP = N_Q_HEADS // N_KV_HEADS
WINDOW = 128
BLOCK = 128
ROPE_THETA = 10000.0
Q_WIDTH = N_Q_HEADS * HEAD_DIM
KV_WIDTH = N_KV_HEADS * HEAD_DIM
POOL_WINDOWS = (2, 4, 8, 16)
N_POOL_GROUPS = len(POOL_WINDOWS)
POOL_WIDTH = D_MODEL // 2
POOL_GROUP_DIM = POOL_WIDTH // N_POOL_GROUPS
Q_OFF = 0
K_OFF = Q_OFF + Q_WIDTH
V_OFF = K_OFF + KV_WIDTH
U_OFF = V_OFF + KV_WIDTH
G_OFF = U_OFF + POOL_WIDTH
IN_WIDTH = G_OFF + 2 * D_MODEL
D_FF = ((8 * D_MODEL // 3 + 255) // 256) * 256
ALPHA = (2.0 * DEPTH) ** 0.25
LN_EPS = 1e-5
NEG_INF = -1e30
SM_SCALE = HEAD_DIM ** -0.5

H_GA = 0
H_GB = H_GA + D_MODEL
H_Q = H_GB + D_MODEL
H_U = H_Q + Q_WIDTH
H_K = H_U + POOL_WIDTH
H_V = H_K + KV_WIDTH

BF16 = jnp.bfloat16
F32 = jnp.float32

V7X_VMEM_BYTES = 64 * 1024 * 1024
VMEM_LIMIT_BYTES = V7X_VMEM_BYTES - 6 * 1024 * 1024
V7X_MXU_DIM = 256
POOL_HALO = 16


def _params(n_axes):
    return pltpu.CompilerParams(
        dimension_semantics=("arbitrary",) * n_axes,
        vmem_limit_bytes=VMEM_LIMIT_BYTES,
    )


def _layer_resident(layer, shape):
    return pl.BlockSpec((None,) + shape, lambda *_: (layer,) + (0,) * len(shape),
                        pipeline_mode=pl.Buffered(1))


def _layer_norm(y, g, b):
    mu = jnp.mean(y, axis=-1, keepdims=True)
    yc = y - mu
    var = jnp.mean(yc * yc, axis=-1, keepdims=True)
    return yc * lax.rsqrt(var + LN_EPS) * g + b


def _in_proj_kernel(x_ref, w_ref, cos_ref, sin_ref, h_ref, xb_ref):
    xb_ref[...] = x_ref[...].astype(BF16)

    def rope(t):
        return t * cos_ref[...] + pltpu.roll(t, HEAD_DIM // 2, 1) * sin_ref[...]

    def per_head(fn, acc):
        return jnp.concatenate(
            [fn(acc[:, c:c + HEAD_DIM]) for c in range(0, acc.shape[1], HEAD_DIM)], axis=1)

    for c0 in range(0, IN_WIDTH, V7X_MXU_DIM):
        cs = slice(c0, c0 + V7X_MXU_DIM)
        acc = jnp.dot(xb_ref[...], w_ref[:, cs], preferred_element_type=F32)
        if c0 < H_Q:
            out = jax.nn.sigmoid(acc)
        elif c0 < H_U:
            out = per_head(lambda t: rope(t) * SM_SCALE, acc)
        elif H_K <= c0 < H_V:
            out = per_head(rope, acc)
        else:
            out = acc
        h_ref[:, cs] = out.astype(BF16)


def _in_proj(x, w, cos, sin, layer, seq, tm):
    t = x.shape[0]
    tiles_per_seq = seq // tm
    return pl.pallas_call(
        _in_proj_kernel,
        grid=(t // tm,),
        in_specs=[
            pl.BlockSpec((tm, D_MODEL), lambda i: (i, 0)),
            _layer_resident(layer, (D_MODEL, IN_WIDTH)),
            pl.BlockSpec((tm, HEAD_DIM), lambda i: (i % tiles_per_seq, 0)),
            pl.BlockSpec((tm, HEAD_DIM), lambda i: (i % tiles_per_seq, 0)),
        ],
        out_specs=pl.BlockSpec((tm, IN_WIDTH), lambda i: (i, 0)),
        out_shape=jax.ShapeDtypeStruct((t, IN_WIDTH), BF16),
        scratch_shapes=[pltpu.VMEM((tm, D_MODEL), BF16)],
        compiler_params=_params(1),
        name="in_proj",
    )(x, w, cos, sin)


def _attn_kernel(sink_ref, q_ref, kvp_ref, kvm_ref, kvn_ref, o_ref, kv_ref, *, layer, tq, tiles_per_seq):
    t = pl.program_id(0) % tiles_per_seq
    is_first = t == 0
    is_last = t == tiles_per_seq - 1
    nb = tq // BLOCK
    kw = 3 * BLOCK

    kv_ref[0:BLOCK, :] = kvp_ref[...]
    kv_ref[BLOCK:BLOCK + tq, :] = kvm_ref[...]
    kv_ref[BLOCK + tq:, :] = kvn_ref[...]

    rows = GROUP * BLOCK
    qi = lax.broadcasted_iota(jnp.int32, (rows, kw), 0) % BLOCK
    kj = lax.broadcasted_iota(jnp.int32, (rows, kw), 1)
    rel = kj - qi
    band = (rel >= BLOCK - WINDOW) & (rel <= BLOCK + WINDOW)
    head_of_row = lax.broadcasted_iota(jnp.int32, (rows, 1), 0) // BLOCK

    for b in range(nb):
        mask = band
        if b == 0:
            mask = mask & (kj >= jnp.where(is_first, BLOCK, 0))
        if b == nb - 1:
            mask = mask & (kj < jnp.where(is_last, 2 * BLOCK, kw))
        rs = slice(b * BLOCK, (b + 1) * BLOCK)
        for g in range(N_KV_HEADS):
            k = kv_ref[b * BLOCK:b * BLOCK + kw, g * HEAD_DIM:(g + 1) * HEAD_DIM]
            v = kv_ref[b * BLOCK:b * BLOCK + kw, KV_WIDTH + g * HEAD_DIM:KV_WIDTH + (g + 1) * HEAD_DIM]
            q = jnp.concatenate(
                [q_ref[rs, (g * GROUP + hh) * HEAD_DIM:(g * GROUP + hh + 1) * HEAD_DIM] for hh in range(GROUP)],
                axis=0)
            sk = jnp.zeros((rows, 1), F32)
            for hh in range(GROUP):
                sk = jnp.where(head_of_row == hh, sink_ref[layer, g * GROUP + hh], sk)
            s = lax.dot_general(q, k, (((1,), (1,)), ((), ())), preferred_element_type=F32)
            s = jnp.where(mask, s, NEG_INF)
            m = jnp.maximum(jnp.max(s, axis=-1, keepdims=True), sk)
            p = jnp.exp(s - m)
            denom = jnp.sum(p, axis=-1, keepdims=True) + jnp.exp(sk - m)
            o = jnp.dot(p.astype(BF16), v, preferred_element_type=F32) / denom
            for hh in range(GROUP):
                h = g * GROUP + hh
                o_ref[rs, h * HEAD_DIM:(h + 1) * HEAD_DIM] = o[hh * BLOCK:(hh + 1) * BLOCK].astype(BF16)


def _win_attn(h, sink, layer, seq, tq):
    t = h.shape[0]
    tiles_per_seq = seq // tq
    nblk = t // BLOCK
    bpt = tq // BLOCK
    kv_col = H_K // (2 * KV_WIDTH)
    return pl.pallas_call(
        functools.partial(_attn_kernel, layer=layer, tq=tq, tiles_per_seq=tiles_per_seq),
        grid=(t // tq,),
        in_specs=[
            pl.BlockSpec(memory_space=pltpu.SMEM),
            pl.BlockSpec((tq, Q_WIDTH), lambda i: (i, H_Q // Q_WIDTH)),
            pl.BlockSpec((BLOCK, 2 * KV_WIDTH), lambda i: (jnp.maximum(i * bpt - 1, 0), kv_col)),
            pl.BlockSpec((tq, 2 * KV_WIDTH), lambda i: (i, kv_col)),
            pl.BlockSpec((BLOCK, 2 * KV_WIDTH), lambda i: (jnp.minimum((i + 1) * bpt, nblk - 1), kv_col)),
        ],
        out_specs=pl.BlockSpec((tq, Q_WIDTH), lambda i: (i, 0)),
        out_shape=jax.ShapeDtypeStruct((t, Q_WIDTH), BF16),
        scratch_shapes=[pltpu.VMEM((tq + 2 * BLOCK, 2 * KV_WIDTH), BF16)],
        compiler_params=_params(1),
        name="win_attn",
    )(sink, h, h, h, h)


def _pool_kernel(up_ref, um_ref, un_ref, wmix_ref, scale_ref, pm_ref, *, tp, seq, tiles_per_seq):
    t = pl.program_id(0) % tiles_per_seq
    n = tp + 2 * POOL_HALO
    u = jnp.concatenate([
        jnp.where(t == 0, 0.0, up_ref[...].astype(F32)),
        um_ref[...].astype(F32),
        jnp.where(t == tiles_per_seq - 1, 0.0, un_ref[...].astype(F32)),
    ], axis=0)

    def back(a, s):
        return pltpu.roll(a, s, 0)

    pos = t * tp + lax.broadcasted_iota(jnp.int32, (tp, 1), 0)
    for g, w in enumerate(POOL_WINDOWS):
        half = w // 2
        cs = slice(g * POOL_GROUP_DIM, (g + 1) * POOL_GROUP_DIM)
        ug = u[:, cs]
        d, span = ug, 1
        while span < w:
            d = d + back(d, span)
            span *= 2
        if half > 1:
            d = back(d, n - (half - 1))
        win = d[POOL_HALO:POOL_HALO + tp]
        cnt = jnp.minimum(pos + half, seq) - jnp.maximum(pos - half, 0)
        inv = 1.0 / cnt.astype(F32)
        pooled = win * inv - ug[POOL_HALO:POOL_HALO + tp]
        mixed = jnp.dot(pooled.astype(BF16), wmix_ref[g], preferred_element_type=F32)
        pm_ref[:, cs] = (mixed * scale_ref[:, cs]).astype(BF16)


def _pool_mix(h, w_mix, scale, layer, seq, tp):
    t = h.shape[0]
    tiles_per_seq = seq // tp
    hpt = tp // POOL_HALO
    nhalo = t // POOL_HALO
    u_col = H_U // POOL_WIDTH
    return pl.pallas_call(
        functools.partial(_pool_kernel, tp=tp, seq=seq, tiles_per_seq=tiles_per_seq),
        grid=(t // tp,),
        in_specs=[
            pl.BlockSpec((POOL_HALO, POOL_WIDTH), lambda i: (jnp.maximum(i * hpt - 1, 0), u_col)),
            pl.BlockSpec((tp, POOL_WIDTH), lambda i: (i, u_col)),
            pl.BlockSpec((POOL_HALO, POOL_WIDTH), lambda i: (jnp.minimum((i + 1) * hpt, nhalo - 1), u_col)),
            _layer_resident(layer, (N_POOL_GROUPS, POOL_GROUP_DIM, POOL_GROUP_DIM)),
            _layer_resident(layer, (1, POOL_WIDTH)),
        ],
        out_specs=pl.BlockSpec((tp, POOL_WIDTH), lambda i: (i, 0)),
        out_shape=jax.ShapeDtypeStruct((t, POOL_WIDTH), BF16),
        compiler_params=_params(1),
        name="pool_mix",
    )(h, h, h, w_mix, scale)


MIX_SUB = 256


def _mix_kernel(o_ref, pm_ref, sga_ref, sgb_ref, x_ref, wap_ref, wpp_ref, wout_ref, g_ref, b_ref, y_ref):
    for r0 in range(0, o_ref.shape[0], MIX_SUB):
        rs = slice(r0, r0 + MIX_SUB)
        a = jnp.dot(o_ref[rs, :], wap_ref[...], preferred_element_type=F32)
        b = jnp.dot(pm_ref[rs, :], wpp_ref[...], preferred_element_type=F32)
        gated = sga_ref[rs, :].astype(F32) * a + sgb_ref[rs, :].astype(F32) * b
        mix = jnp.dot(gated.astype(BF16), wout_ref[...], preferred_element_type=F32)
        y_ref[rs, :] = _layer_norm(ALPHA * x_ref[rs, :] + mix, g_ref[...], b_ref[...])


def _mix_ln(o, pm, h, x, wap, wpp, wout, g, b, layer, tm):
    t = x.shape[0]
    row = lambda i: (i, 0)
    return pl.pallas_call(
        _mix_kernel,
        grid=(t // tm,),
        in_specs=[
            pl.BlockSpec((tm, Q_WIDTH), row),
            pl.BlockSpec((tm, POOL_WIDTH), row),
            pl.BlockSpec((tm, D_MODEL), lambda i: (i, H_GA // D_MODEL)),
            pl.BlockSpec((tm, D_MODEL), lambda i: (i, H_GB // D_MODEL)),
            pl.BlockSpec((tm, D_MODEL), row),
            _layer_resident(layer, (Q_WIDTH, D_MODEL)),
            _layer_resident(layer, (POOL_WIDTH, D_MODEL)),
            _layer_resident(layer, (D_MODEL, D_MODEL)),
            _layer_resident(layer, (1, D_MODEL)),
            _layer_resident(layer, (1, D_MODEL)),
        ],
        out_specs=pl.BlockSpec((tm, D_MODEL), row),
        out_shape=jax.ShapeDtypeStruct((t, D_MODEL), F32),
        compiler_params=_params(1),
        name="mix_ln",
    )(o, pm, h, h, x, wap, wpp, wout, g, b)


FFN_TF = 512
_FFN_STEPS = D_FF // FFN_TF
FFN_SUB = 256


def _ffn_kernel(x_ref, wg_ref, wu_ref, wd_ref, g_ref, b_ref, y_ref):
    c = pl.program_id(1)
    for r0 in range(0, x_ref.shape[0], FFN_SUB):
        rs = slice(r0, r0 + FFN_SUB)
        xb = x_ref[rs, :].astype(BF16)
        gate = jnp.dot(xb, wg_ref[...], preferred_element_type=F32)
        up = jnp.dot(xb, wu_ref[...], preferred_element_type=F32)
        mid = (gate * jax.nn.sigmoid(gate) * up).astype(BF16)
        part = jnp.dot(mid, wd_ref[...], preferred_element_type=F32)
        y_ref[rs, :] = jnp.where(c == 0, 0.0, y_ref[rs, :]) + part

    @pl.when(c == _FFN_STEPS - 1)
    def _():
        for r0 in range(0, x_ref.shape[0], FFN_SUB):
            rs = slice(r0, r0 + FFN_SUB)
            y_ref[rs, :] = _layer_norm(ALPHA * x_ref[rs, :] + y_ref[rs, :], g_ref[...], b_ref[...])


def _ffn_ln(x, w_in, w_out, g, b, layer, tm):
    t = x.shape[0]
    row = lambda i, c: (i, 0)
    return pl.pallas_call(
        _ffn_kernel,
        grid=(t // tm, _FFN_STEPS),
        in_specs=[
            pl.BlockSpec((tm, D_MODEL), row),
            pl.BlockSpec((None, D_MODEL, FFN_TF), lambda i, c: (layer, 0, c)),
            pl.BlockSpec((None, D_MODEL, FFN_TF), lambda i, c: (layer, 0, _FFN_STEPS + c)),
            pl.BlockSpec((None, FFN_TF, D_MODEL), lambda i, c: (layer, c, 0)),
            _layer_resident(layer, (1, D_MODEL)),
            _layer_resident(layer, (1, D_MODEL)),
        ],
        out_specs=pl.BlockSpec((tm, D_MODEL), row),
        out_shape=jax.ShapeDtypeStruct((t, D_MODEL), F32),
        compiler_params=_params(2),
        name="ffn_ln",
    )(x, w_in, w_in, w_out, g, b)


TILES = dict(in_proj=512, win_attn=512, pool_mix=512, mix_ln=512, ffn_ln=1024)


def _rope_tables(seq):
    half = HEAD_DIM // 2
    inv_freq = ROPE_THETA ** (-jnp.arange(half, dtype=F32) / half)
    ang = jnp.arange(seq, dtype=F32)[:, None] * inv_freq[None, :]
    cos, sin = jnp.cos(ang), jnp.sin(ang)
    return jnp.concatenate([cos, cos], axis=1), jnp.concatenate([-sin, sin], axis=1)


def _trunk(x, p, tiles):
    bsz, seq, _ = x.shape
    x = x.reshape(bsz * seq, D_MODEL)
    cos, sin = _rope_tables(seq)
    for l in range(p["w_in"].shape[0]):
        h = _in_proj(x, p["w_in"], cos, sin, l, seq, tiles["in_proj"])
        o = _win_attn(h, p["sink"], l, seq, tiles["win_attn"])
        pm = _pool_mix(h, p["w_pool_mix"], p["pool_scale"], l, seq, tiles["pool_mix"])
        x = _mix_ln(o, pm, h, x, p["w_attn_proj"], p["w_pool_proj"], p["w_out"],
                    p["ln1_g"], p["ln1_b"], l, tiles["mix_ln"])
        x = _ffn_ln(x, p["w_ffn_in"], p["w_ffn_out"], p["ln2_g"], p["ln2_b"], l, tiles["ffn_ln"])
    return x.reshape(bsz, seq, D_MODEL)


def _prepare_params(w_in, sink, w_attn_proj, w_pool_mix, pool_scale, w_pool_proj, w_out,
                    ln1_g, ln1_b, w_ffn_in, w_ffn_out, ln2_g, ln2_b):
    w_in_p = jnp.concatenate(
        [w_in[:, :, G_OFF:], w_in[:, :, Q_OFF:K_OFF], w_in[:, :, U_OFF:G_OFF], w_in[:, :, K_OFF:U_OFF]],
        axis=2).astype(BF16)
    depth = w_in.shape[0]
    vec = lambda a: a.astype(F32).reshape(depth, 1, -1)
    return {
        "w_in": w_in_p,
        "sink": sink.astype(F32),
        "w_attn_proj": w_attn_proj.astype(BF16),
        "w_pool_mix": w_pool_mix.astype(BF16),
        "pool_scale": vec(pool_scale),
        "w_pool_proj": w_pool_proj.astype(BF16),
        "w_out": w_out.astype(BF16),
        "ln1_g": vec(ln1_g), "ln1_b": vec(ln1_b),
        "w_ffn_in": w_ffn_in.astype(BF16),
        "w_ffn_out": w_ffn_out.astype(BF16),
        "ln2_g": vec(ln2_g), "ln2_b": vec(ln2_b),
    }


def kernel(x_prompt, x_sample, w_in, sink, w_attn_proj, w_pool_mix, pool_scale, w_pool_proj, w_out,
           ln1_g, ln1_b, w_ffn_in, w_ffn_out, ln2_g, ln2_b):
    p = _prepare_params(w_in, sink, w_attn_proj, w_pool_mix, pool_scale, w_pool_proj, w_out,
                        ln1_g, ln1_b, w_ffn_in, w_ffn_out, ln2_g, ln2_b)
    return (_trunk(x_prompt, p, TILES), _trunk(x_sample, p, TILES))
```

```python
import functools

import jax
import jax.numpy as jnp
from jax import lax
from jax.experimental import pallas as pl
from jax.experimental.pallas import tpu as pltpu

D_MODEL = 2048
DEPTH = 2
HEAD_DIM = 128
N_Q_HEADS = 8
N_KV_HEADS = 2
GROUP = N_Q_HEADS // N_KV_HEADS
WINDOW = 128
BLOCK = 128
ROPE_THETA = 10000.0
Q_WIDTH = N_Q_HEADS * HEAD_DIM
KV_WIDTH = N_KV_HEADS * HEAD_DIM
POOL_WINDOWS = (2, 4, 8, 16)
N_POOL_GROUPS = len(POOL_WINDOWS)
POOL_WIDTH = D_MODEL // 2
POOL_GROUP_DIM = POOL_WIDTH // N_POOL_GROUPS
Q_OFF = 0
K_OFF = Q_OFF + Q_WIDTH
V_OFF = K_OFF + KV_WIDTH
U_OFF = V_OFF + KV_WIDTH
G_OFF = U_OFF + POOL_WIDTH
IN_WIDTH = G_OFF + 2 * D_MODEL
D_FF = ((8 * D_MODEL // 3 + 255) // 256) * 256
ALPHA = (2.0 * DEPTH) ** 0.25
LN_EPS = 1e-5
NEG_INF = -1e30
SM_SCALE = HEAD_DIM ** -0.5
LOG2_E = 1.4426950408889634

H_GA = 0
H_GB = H_GA + D_MODEL
H_Q = H_GB + D_MODEL
H_U = H_Q + Q_WIDTH
H_K = H_U + POOL_WIDTH
H_V = H_K + KV_WIDTH

BF16 = jnp.bfloat16
F32 = jnp.float32

V7X_VMEM_BYTES = 64 * 1024 * 1024
VMEM_LIMIT_BYTES = V7X_VMEM_BYTES - 6 * 1024 * 1024
V7X_MXU_DIM = 256
POOL_HALO = 16


def _params(n_axes):
    return pltpu.CompilerParams(
        dimension_semantics=("arbitrary",) * n_axes,
        vmem_limit_bytes=VMEM_LIMIT_BYTES,
    )


def _layer_resident(layer, shape):
    return pl.BlockSpec((None,) + shape, lambda *_: (layer,) + (0,) * len(shape),
                        pipeline_mode=pl.Buffered(1))


def _layer_norm(y, g, b):
    mu = jnp.mean(y, axis=-1, keepdims=True)
    yc = y - mu
    var = jnp.mean(yc * yc, axis=-1, keepdims=True)
    return yc * lax.rsqrt(var + LN_EPS) * g + b


def _w_in_col(h_col):
    for h0, w0, width in ((H_GA, G_OFF, 2 * D_MODEL), (H_Q, Q_OFF, Q_WIDTH), (H_U, U_OFF, POOL_WIDTH),
                          (H_K, K_OFF, 2 * KV_WIDTH)):
        if h0 <= h_col < h0 + width:
            return w0 + h_col - h0
    raise ValueError(h_col)


def _in_proj_kernel(x_ref, w_ref, cos_ref, sin_ref, h_ref, xb_ref):
    xb_ref[...] = x_ref[...].astype(BF16)

    def rope(t):
        return t * cos_ref[...] + pltpu.roll(t, HEAD_DIM // 2, 1) * sin_ref[...]

    def per_head(fn, acc):
        return jnp.concatenate(
            [fn(acc[:, c:c + HEAD_DIM]) for c in range(0, acc.shape[1], HEAD_DIM)], axis=1)

    for c0 in range(0, IN_WIDTH, V7X_MXU_DIM):
        cs = slice(c0, c0 + V7X_MXU_DIM)
        w0 = _w_in_col(c0)
        acc = jnp.dot(xb_ref[...], w_ref[:, w0:w0 + V7X_MXU_DIM], preferred_element_type=F32)
        if c0 < H_Q:
            out = jax.nn.sigmoid(acc)
        elif c0 < H_U:
            out = per_head(lambda t: rope(t) * (SM_SCALE * LOG2_E), acc)
        elif H_K <= c0 < H_V:
            out = per_head(rope, acc)
        else:
            out = acc
        h_ref[:, cs] = out.astype(BF16)


def _in_proj(x, w, cos, sin, layer, seq, tm):
    t = x.shape[0]
    tiles_per_seq = seq // tm
    return pl.pallas_call(
        _in_proj_kernel,
        grid=(t // tm,),
        in_specs=[
            pl.BlockSpec((tm, D_MODEL), lambda i: (i, 0)),
            _layer_resident(layer, (D_MODEL, IN_WIDTH)),
            pl.BlockSpec((tm, HEAD_DIM), lambda i: (i % tiles_per_seq, 0)),
            pl.BlockSpec((tm, HEAD_DIM), lambda i: (i % tiles_per_seq, 0)),
        ],
        out_specs=pl.BlockSpec((tm, IN_WIDTH), lambda i: (i, 0)),
        out_shape=jax.ShapeDtypeStruct((t, IN_WIDTH), BF16),
        scratch_shapes=[pltpu.VMEM((tm, D_MODEL), BF16)],
        compiler_params=_params(1),
        name="in_proj",
    )(x, w, cos, sin)


def _attn_kernel(sink_ref, q_ref, kvp_ref, kvm_ref, kvn_ref, o_ref, kv_ref, *, layer, tq, tiles_per_seq):
    t = pl.program_id(0) % tiles_per_seq
    is_first = t == 0
    is_last = t == tiles_per_seq - 1
    nb = tq // BLOCK
    kw = 3 * BLOCK

    kv_ref[0:BLOCK, :] = kvp_ref[...]
    kv_ref[BLOCK:BLOCK + tq, :] = kvm_ref[...]
    kv_ref[BLOCK + tq:, :] = kvn_ref[...]

    rows = GROUP * BLOCK
    qi = lax.broadcasted_iota(jnp.int32, (rows, kw), 0) % BLOCK
    kj = lax.broadcasted_iota(jnp.int32, (rows, kw), 1)
    rel = kj - qi
    band = (rel >= BLOCK - WINDOW) & (rel <= BLOCK + WINDOW)
    head_of_row = lax.broadcasted_iota(jnp.int32, (rows, 1), 0) // BLOCK

    for b in range(nb):
        mask = band
        if b == 0:
            mask = mask & (kj >= jnp.where(is_first, BLOCK, 0))
        if b == nb - 1:
            mask = mask & (kj < jnp.where(is_last, 2 * BLOCK, kw))
        rs = slice(b * BLOCK, (b + 1) * BLOCK)
        for g in range(N_KV_HEADS):
            k = kv_ref[b * BLOCK:b * BLOCK + kw, g * HEAD_DIM:(g + 1) * HEAD_DIM]
            v = kv_ref[b * BLOCK:b * BLOCK + kw, KV_WIDTH + g * HEAD_DIM:KV_WIDTH + (g + 1) * HEAD_DIM]
            q = jnp.concatenate(
                [q_ref[rs, (g * GROUP + hh) * HEAD_DIM:(g * GROUP + hh + 1) * HEAD_DIM] for hh in range(GROUP)],
                axis=0)
            sk = jnp.zeros((rows, 1), F32)
            for hh in range(GROUP):
                sk = jnp.where(head_of_row == hh, sink_ref[layer, g * GROUP + hh] * LOG2_E, sk)
            s = lax.dot_general(q, k, (((1,), (1,)), ((), ())), preferred_element_type=F32)
            s = jnp.where(mask, s, NEG_INF)
            m = jnp.maximum(jnp.max(s, axis=-1, keepdims=True), sk)
            p = jnp.exp2(s - m)
            denom = jnp.sum(p, axis=-1, keepdims=True) + jnp.exp2(sk - m)
            o = jnp.dot(p.astype(BF16), v, preferred_element_type=F32) / denom
            for hh in range(GROUP):
                h = g * GROUP + hh
                o_ref[rs, h * HEAD_DIM:(h + 1) * HEAD_DIM] = o[hh * BLOCK:(hh + 1) * BLOCK].astype(BF16)


def _win_attn(h, sink, layer, seq, tq):
    t = h.shape[0]
    tiles_per_seq = seq // tq
    nblk = t // BLOCK
    bpt = tq // BLOCK
    kv_col = H_K // (2 * KV_WIDTH)
    return pl.pallas_call(
        functools.partial(_attn_kernel, layer=layer, tq=tq, tiles_per_seq=tiles_per_seq),
        grid=(t // tq,),
        in_specs=[
            pl.BlockSpec(memory_space=pltpu.SMEM),
            pl.BlockSpec((tq, Q_WIDTH), lambda i: (i, H_Q // Q_WIDTH)),
            pl.BlockSpec((BLOCK, 2 * KV_WIDTH), lambda i: (jnp.maximum(i * bpt - 1, 0), kv_col)),
            pl.BlockSpec((tq, 2 * KV_WIDTH), lambda i: (i, kv_col)),
            pl.BlockSpec((BLOCK, 2 * KV_WIDTH), lambda i: (jnp.minimum((i + 1) * bpt, nblk - 1), kv_col)),
        ],
        out_specs=pl.BlockSpec((tq, Q_WIDTH), lambda i: (i, 0)),
        out_shape=jax.ShapeDtypeStruct((t, Q_WIDTH), BF16),
        scratch_shapes=[pltpu.VMEM((tq + 2 * BLOCK, 2 * KV_WIDTH), BF16)],
        compiler_params=_params(1),
        name="win_attn",
    )(sink, h, h, h, h)


def _pool_kernel(up_ref, um_ref, un_ref, wmix_ref, scale_ref, pm_ref, *, tp, seq, tiles_per_seq):
    t = pl.program_id(0) % tiles_per_seq
    n = tp + 2 * POOL_HALO
    u = jnp.concatenate([
        jnp.where(t == 0, 0.0, up_ref[...].astype(F32)),
        um_ref[...].astype(F32),
        jnp.where(t == tiles_per_seq - 1, 0.0, un_ref[...].astype(F32)),
    ], axis=0)

    def back(a, s):
        return pltpu.roll(a, s, 0)

    pos = t * tp + lax.broadcasted_iota(jnp.int32, (tp, 1), 0)
    for g, w in enumerate(POOL_WINDOWS):
        half = w // 2
        cs = slice(g * POOL_GROUP_DIM, (g + 1) * POOL_GROUP_DIM)
        ug = u[:, cs]
        d, span = ug, 1
        while span < w:
            d = d + back(d, span)
            span *= 2
        if half > 1:
            d = back(d, n - (half - 1))
        win = d[POOL_HALO:POOL_HALO + tp]
        cnt = jnp.minimum(pos + half, seq) - jnp.maximum(pos - half, 0)
        inv = 1.0 / cnt.astype(F32)
        pooled = win * inv - ug[POOL_HALO:POOL_HALO + tp]
        mixed = jnp.dot(pooled.astype(BF16), wmix_ref[g], preferred_element_type=F32)
        pm_ref[:, cs] = (mixed * scale_ref[:, cs]).astype(BF16)


def _pool_mix(h, w_mix, scale, layer, seq, tp):
    t = h.shape[0]
    tiles_per_seq = seq // tp
    hpt = tp // POOL_HALO
    nhalo = t // POOL_HALO
    u_col = H_U // POOL_WIDTH
    return pl.pallas_call(
        functools.partial(_pool_kernel, tp=tp, seq=seq, tiles_per_seq=tiles_per_seq),
        grid=(t // tp,),
        in_specs=[
            pl.BlockSpec((POOL_HALO, POOL_WIDTH), lambda i: (jnp.maximum(i * hpt - 1, 0), u_col)),
            pl.BlockSpec((tp, POOL_WIDTH), lambda i: (i, u_col)),
            pl.BlockSpec((POOL_HALO, POOL_WIDTH), lambda i: (jnp.minimum((i + 1) * hpt, nhalo - 1), u_col)),
            _layer_resident(layer, (N_POOL_GROUPS, POOL_GROUP_DIM, POOL_GROUP_DIM)),
            _layer_resident(layer, (1, POOL_WIDTH)),
        ],
        out_specs=pl.BlockSpec((tp, POOL_WIDTH), lambda i: (i, 0)),
        out_shape=jax.ShapeDtypeStruct((t, POOL_WIDTH), BF16),
        compiler_params=_params(1),
        name="pool_mix",
    )(h, h, h, w_mix, scale)


MIX_SUB = 256


def _mix_kernel(o_ref, pm_ref, sga_ref, sgb_ref, x_ref, wap_ref, wpp_ref, wout_ref, g_ref, b_ref, y_ref):
    for r0 in range(0, o_ref.shape[0], MIX_SUB):
        rs = slice(r0, r0 + MIX_SUB)
        a = jnp.dot(o_ref[rs, :], wap_ref[...], preferred_element_type=F32)
        b = jnp.dot(pm_ref[rs, :], wpp_ref[...], preferred_element_type=F32)
        gated = sga_ref[rs, :].astype(F32) * a + sgb_ref[rs, :].astype(F32) * b
        mix = jnp.dot(gated.astype(BF16), wout_ref[...], preferred_element_type=F32)
        y_ref[rs, :] = _layer_norm(ALPHA * x_ref[rs, :] + mix, g_ref[...], b_ref[...])


def _mix_ln(o, pm, h, x, wap, wpp, wout, g, b, layer, tm):
    t = x.shape[0]
    row = lambda i: (i, 0)
    return pl.pallas_call(
        _mix_kernel,
        grid=(t // tm,),
        in_specs=[
            pl.BlockSpec((tm, Q_WIDTH), row),
            pl.BlockSpec((tm, POOL_WIDTH), row),
            pl.BlockSpec((tm, D_MODEL), lambda i: (i, H_GA // D_MODEL)),
            pl.BlockSpec((tm, D_MODEL), lambda i: (i, H_GB // D_MODEL)),
            pl.BlockSpec((tm, D_MODEL), row),
            _layer_resident(layer, (Q_WIDTH, D_MODEL)),
            _layer_resident(layer, (POOL_WIDTH, D_MODEL)),
            _layer_resident(layer, (D_MODEL, D_MODEL)),
            _layer_resident(layer, (1, D_MODEL)),
            _layer_resident(layer, (1, D_MODEL)),
        ],
        out_specs=pl.BlockSpec((tm, D_MODEL), row),
        out_shape=jax.ShapeDtypeStruct((t, D_MODEL), F32),
        compiler_params=_params(1),
        name="mix_ln",
    )(o, pm, h, h, x, wap, wpp, wout, g, b)


FFN_TF = 512
_FFN_STEPS = D_FF // FFN_TF
FFN_SUB = 512
FFN_LAST_SUB = 256


def _ffn_kernel(x_ref, wg_ref, wu_ref, wd_ref, g_ref, b_ref, y_ref):
    c = pl.program_id(1)

    def chunk(rs):
        xb = x_ref[rs, :].astype(BF16)
        gate = jnp.dot(xb, wg_ref[...], preferred_element_type=F32)
        up = jnp.dot(xb, wu_ref[...], preferred_element_type=F32)
        mid = (gate * jax.nn.sigmoid(gate) * up).astype(BF16)
        return jnp.dot(mid, wd_ref[...], preferred_element_type=F32)

    @pl.when(c < _FFN_STEPS - 1)
    def _():
        for r0 in range(0, x_ref.shape[0], FFN_SUB):
            rs = slice(r0, r0 + FFN_SUB)
            y_ref[rs, :] = jnp.where(c == 0, 0.0, y_ref[rs, :]) + chunk(rs)

    @pl.when(c == _FFN_STEPS - 1)
    def _():
        for r0 in range(0, x_ref.shape[0], FFN_LAST_SUB):
            rs = slice(r0, r0 + FFN_LAST_SUB)
            f = y_ref[rs, :] + chunk(rs)
            y_ref[rs, :] = _layer_norm(ALPHA * x_ref[rs, :] + f, g_ref[...], b_ref[...])


def _ffn_ln(x, w_in, w_out, g, b, layer, tm):
    t = x.shape[0]
    row = lambda i, c: (i, 0)
    return pl.pallas_call(
        _ffn_kernel,
        grid=(t // tm, _FFN_STEPS),
        in_specs=[
            pl.BlockSpec((tm, D_MODEL), row),
            pl.BlockSpec((None, D_MODEL, FFN_TF), lambda i, c: (layer, 0, c)),
            pl.BlockSpec((None, D_MODEL, FFN_TF), lambda i, c: (layer, 0, _FFN_STEPS + c)),
            pl.BlockSpec((None, FFN_TF, D_MODEL), lambda i, c: (layer, c, 0)),
            _layer_resident(layer, (1, D_MODEL)),
            _layer_resident(layer, (1, D_MODEL)),
        ],
        out_specs=pl.BlockSpec((tm, D_MODEL), row),
        out_shape=jax.ShapeDtypeStruct((t, D_MODEL), F32),
        compiler_params=_params(2),
        name="ffn_ln",
    )(x, w_in, w_in, w_out, g, b)


TILES = dict(in_proj=512, win_attn=512, pool_mix=512, mix_ln=512, ffn_ln=1024)


def _rope_tables(seq):
    half = HEAD_DIM // 2
    inv_freq = ROPE_THETA ** (-jnp.arange(half, dtype=F32) / half)
    ang = jnp.arange(seq, dtype=F32)[:, None] * inv_freq[None, :]
    cos, sin = jnp.cos(ang), jnp.sin(ang)
    return jnp.concatenate([cos, cos], axis=1), jnp.concatenate([-sin, sin], axis=1)


def _trunk(x, p, tiles):
    bsz, seq, _ = x.shape
    x = x.reshape(bsz * seq, D_MODEL)
    cos, sin = _rope_tables(seq)
    for l in range(p["w_in"].shape[0]):
        h = _in_proj(x, p["w_in"], cos, sin, l, seq, tiles["in_proj"])
        o = _win_attn(h, p["sink"], l, seq, tiles["win_attn"])
        pm = _pool_mix(h, p["w_pool_mix"], p["pool_scale"], l, seq, tiles["pool_mix"])
        x = _mix_ln(o, pm, h, x, p["w_attn_proj"], p["w_pool_proj"], p["w_out"],
                    p["ln1_g"], p["ln1_b"], l, tiles["mix_ln"])
        x = _ffn_ln(x, p["w_ffn_in"], p["w_ffn_out"], p["ln2_g"], p["ln2_b"], l, tiles["ffn_ln"])
    return x.reshape(bsz, seq, D_MODEL)


def _prepare_params(w_in, sink, w_attn_proj, w_pool_mix, pool_scale, w_pool_proj, w_out,
                    ln1_g, ln1_b, w_ffn_in, w_ffn_out, ln2_g, ln2_b):
    depth = w_in.shape[0]
    vec = lambda a: a.astype(F32).reshape(depth, 1, -1)
    return {
        "w_in": w_in.astype(BF16),
        "sink": sink.astype(F32),
        "w_attn_proj": w_attn_proj.astype(BF16),
        "w_pool_mix": w_pool_mix.astype(BF16),
        "pool_scale": vec(pool_scale),
        "w_pool_proj": w_pool_proj.astype(BF16),
        "w_out": w_out.astype(BF16),
        "ln1_g": vec(ln1_g), "ln1_b": vec(ln1_b),
        "w_ffn_in": w_ffn_in.astype(BF16),
        "w_ffn_out": w_ffn_out.astype(BF16),
        "ln2_g": vec(ln2_g), "ln2_b": vec(ln2_b),
    }


def kernel(x_prompt, x_sample, w_in, sink, w_attn_proj, w_pool_mix, pool_scale, w_pool_proj, w_out,
           ln1_g, ln1_b, w_ffn_in, w_ffn_out, ln2_g, ln2_b):
    p = _prepare_params(w_in, sink, w_attn_proj, w_pool_mix, pool_scale, w_pool_proj, w_out,
                        ln1_g, ln1_b, w_ffn_in, w_ffn_out, ln2_g, ln2_b)
    return (_trunk(x_prompt, p, TILES), _trunk(x_sample, p, TILES))
```
